```python
import math
import jax, jax.numpy as jnp
from jax import lax
import numpy as np

D_MODEL = 2048
BATCH = 4
SEQ = 2048
DEPTH = 4
DEC_BATCH = 128
DEC_SEQ = 1
PAST_LEN = 16384
PAGE_SIZE = 128

PLE_DIM = 256
CONV_W = 4
CHUNK = 64
EPS = 1e-6
H_A = 4
DK_A = 128
DV_A = 128
W_A = H_A * DV_A
W_B = 512
NB_B = 8
BW_B = W_B // NB_B
LRU_C = 8.0
H_C = 8
P_C = 64
W_C = H_C * P_C
G_C = 1
N_C = 128
XBC_C = W_C + 2 * G_C * N_C
H_D = 4
DK_D = 64
DV_D = 128
W_D = H_D * DV_D
R_D = 16
GLA_TAU = 16.0
W_MIX = W_A + W_B + W_C + W_D
D_FF = 5632
N_EXP = 8
TOP_K = 2
D_FF_EXP = 2816
IN_SPLITS = (H_A * DK_A, H_A * DK_A, W_A, W_A, H_A, H_A,
             W_B, W_B,
             W_C, XBC_C, H_C,
             H_D * DK_D, H_D * DK_D, W_D, W_D, R_D)
IN_W = sum(IN_SPLITS)
N_STATE = 8

kernel_name = 'hybrid_mlstm_rglru_ssd_gla_step'


def _split(x, sizes):
    out, o = [], 0
    for s in sizes:
        out.append(x[..., o:o + s])
        o += s
    return out


def _rmsnorm(x, g):
    xf = x.astype(jnp.float32)
    y = xf * lax.rsqrt(jnp.mean(xf * xf, axis=-1, keepdims=True) + EPS)
    return (y * g.astype(jnp.float32)).astype(x.dtype)


def _head_rmsnorm(x, g):
    b, s, h, d = x.shape
    xf = x.astype(jnp.float32)
    y = xf * lax.rsqrt(jnp.mean(xf * xf, axis=-1, keepdims=True) + EPS)
    return y.reshape(b, s, h * d) * g.astype(jnp.float32)


def _chunk_len(s):
    return CHUNK if s % CHUNK == 0 else s


def _to_chunks(x, L):
    b, s = x.shape[:2]
    return jnp.moveaxis(x.reshape((b, s // L, L) + x.shape[2:]), 1, 0)


def _from_chunks(x):
    x = jnp.moveaxis(x, 0, 1)
    return x.reshape((x.shape[0], x.shape[1] * x.shape[2]) + x.shape[3:])


def _causal_conv(x, buf, w, b):
    s = x.shape[1]
    xp = jnp.concatenate([buf.astype(x.dtype), x], axis=1)
    y = b + xp[:, 0:s] * w[0]
    for k in range(1, CONV_W):
        y = y + xp[:, k:k + s] * w[k]
    return y, xp[:, s:]


def _mlstm(q, k, v, ig, fg, c0, n0, m0):
    f32 = jnp.float32
    s = q.shape[1]
    L = _chunk_len(s)
    q = q.astype(f32) * DK_A ** -0.5
    k = k.astype(f32)
    v = v.astype(f32)
    ig = ig.astype(f32)
    lf = jax.nn.log_sigmoid(fg.astype(f32))
    causal = jnp.tril(jnp.ones((L, L), dtype=bool))

    def step(carry, xs):
        c, n, m = carry
        qc, kc, vc, ic, lfc = xs
        F = jnp.cumsum(lfc, axis=1)
        mt = F + jnp.maximum(m[:, None, :], lax.cummax(ic - F, axis=1))
        Fh, ih, mh = (jnp.swapaxes(t, 1, 2) for t in (F, ic, mt))
        logd = Fh[..., :, None] - Fh[..., None, :] + ih[..., None, :] - mh[..., :, None]
        dmat = jnp.exp(jnp.where(causal, logd, -jnp.inf))
        sc = jnp.einsum('bthd,bshd->bhts', qc, kc) * dmat
        inter = jnp.exp(Fh + m[..., None] - mh)
        num = (jnp.einsum('bhts,bshv->bthv', sc, vc)
               + jnp.einsum('bht,bthd,bhdv->bthv', inter, qc, c))
        den = jnp.sum(sc, axis=-1) + inter * jnp.einsum('bthd,bhd->bht', qc, n)
        den = jnp.maximum(jnp.abs(den), jnp.exp(-mh))
        hc = num / jnp.swapaxes(den, 1, 2)[..., None]
        mL = mt[:, -1]
        wk = jnp.exp(F[:, -1:] - F + ic - mL[:, None])
        d0 = jnp.exp(F[:, -1] + m - mL)
        c_new = d0[..., None, None] * c + jnp.einsum('bsh,bshd,bshv->bhdv', wk, kc, vc)
        n_new = d0[..., None] * n + jnp.einsum('bsh,bshd->bhd', wk, kc)
        return (c_new, n_new, mL), hc

    xs = tuple(_to_chunks(t, L) for t in (q, k, v, ig, lf))
    (c1, n1, m1), h = lax.scan(step, (c0.astype(f32), n0.astype(f32), m0.astype(f32)), xs)
    return _from_chunks(h), c1, n1, m1


def _rglru(x, h0, w_a, b_a, w_x, b_x, lam):
    f32 = jnp.float32
    b, s, _ = x.shape
    xf = x.astype(f32)
    xb = xf.reshape(b, s, NB_B, BW_B)
    r = jax.nn.sigmoid(jnp.einsum('bsni,nij->bsnj', xb, w_a).reshape(b, s, W_B) + b_a)
    i = jax.nn.sigmoid(jnp.einsum('bsni,nij->bsnj', xb, w_x).reshape(b, s, W_B) + b_x)
    log_a = -LRU_C * r * jax.nn.softplus(-lam.astype(f32))
    a = jnp.exp(log_a)
    u = jnp.sqrt(-jnp.expm1(2.0 * log_a)) * (i * xf)
    u = u.at[:, 0].add(a[:, 0] * h0.astype(f32))

    def comb(l, r_):
        return (l[0] * r_[0], r_[0] * l[1] + r_[1])

    _, h = lax.associative_scan(comb, (a, u), axis=1)
    return h, h[:, -1]


def _ssd(x, dt, A, Bm, Cm, h0):
    f32 = jnp.float32
    s = x.shape[1]
    L = _chunk_len(s)
    la = dt * A
    xdt = x.astype(f32) * dt[..., None]
    Bh = jnp.repeat(Bm.astype(f32), H_C // G_C, axis=2)
    Ch = jnp.repeat(Cm.astype(f32), H_C // G_C, axis=2)
    causal = jnp.tril(jnp.ones((L, L), dtype=bool))

    def step(h, xs):
        xc, ac, bc, cc = xs
        cum = jnp.cumsum(ac, axis=1)
        ch = jnp.swapaxes(cum, 1, 2)
        dec = jnp.exp(jnp.where(causal, ch[..., :, None] - ch[..., None, :], -jnp.inf))
        cb = jnp.einsum('bthn,bshn->bhts', cc, bc) * dec
        y = (jnp.einsum('bhts,bshp->bthp', cb, xc)
             + jnp.einsum('bthn,bhpn,bht->bthp', cc, h, jnp.exp(ch)))
        wL = jnp.exp(cum[:, -1:] - cum)
        h_new = (jnp.exp(cum[:, -1])[..., None, None] * h
                 + jnp.einsum('bsh,bshn,bshp->bhpn', wL, bc, xc))
        return h_new, y

    xs = tuple(_to_chunks(t, L) for t in (xdt, la, Bh, Ch))
    h1, y = lax.scan(step, h0.astype(f32), xs)
    return _from_chunks(y), h1


def _gla(q, k, v, lg, s0):
    f32 = jnp.float32
    s = q.shape[1]
    L = _chunk_len(s)
    q = q.astype(f32) * DK_D ** -0.5
    k = k.astype(f32)
    v = v.astype(f32)
    causal5 = jnp.tril(jnp.ones((L, L), dtype=bool))[None, :, :, None, None]

    def step(S, xs):
        qc, kc, vc, gc = xs
        bc = jnp.cumsum(gc, axis=1)
        dec = jnp.exp(jnp.where(causal5, bc[:, :, None] - bc[:, None, :], -jnp.inf))
        att = jnp.einsum('bthk,bshk,btshk->bhts', qc, kc, dec)
        o = (jnp.einsum('bhts,bshv->bthv', att, vc)
             + jnp.einsum('bthk,bhkv->bthv', qc * jnp.exp(bc), S))
        bL = bc[:, -1]
        S_new = (jnp.exp(bL)[..., None] * S
                 + jnp.einsum('bshk,bshv->bhkv', kc * jnp.exp(bL[:, None] - bc), vc))
        return S_new, o

    xs = tuple(_to_chunks(t, L) for t in (q, k, v, lg.astype(f32)))
    s1, o = lax.scan(step, s0.astype(f32), xs)
    return _from_chunks(o), s1


def _swiglu(x, wg, wu, wd):
    return (jax.nn.silu(x @ wg) * (x @ wu)) @ wd


def _moe(x, w_router, wg, wu, wd):
    f32 = jnp.float32
    logits = (x @ w_router).astype(f32)
    top_v, top_i = lax.top_k(logits, TOP_K)
    gate = jax.nn.softmax(top_v, axis=-1)
    comb = jnp.sum(jax.nn.one_hot(top_i, N_EXP, dtype=f32) * gate[..., None], axis=-2)
    out = jnp.zeros(x.shape, f32)
    for e in range(N_EXP):
        out = out + comb[..., e:e + 1] * _swiglu(x, wg[e], wu[e], wd[e])
    return out.astype(x.dtype)


def _zero_state(b):
    f32 = jnp.float32
    return (jnp.zeros((b, H_A, DK_A, DV_A), f32), jnp.zeros((b, H_A, DK_A), f32),
            jnp.zeros((b, H_A), f32), jnp.zeros((b, W_B), f32),
            jnp.zeros((b, CONV_W - 1, W_B), f32), jnp.zeros((b, H_C, P_C, N_C), f32),
            jnp.zeros((b, CONV_W - 1, XBC_C), f32), jnp.zeros((b, H_D, DK_D, DV_D), f32))


def _layer(h, p_i, st, lw, ffn_w, use_moe):
    f32 = jnp.float32
    b, s, _ = h.shape
    mc, mn, mm, rh, rconv, sh, sconv, gs = st
    hn = _rmsnorm(h, lw['norm_mix_g'])
    (qa, ka, va, oa, iga, fga, xb, yb, zc, xbc, dtc,
     qd, kd, vd, gd, lrd) = _split(hn @ lw['w_in'], IN_SPLITS)
    ha, mc, mn, mm = _mlstm(qa.reshape(b, s, H_A, DK_A), ka.reshape(b, s, H_A, DK_A),
                            va.reshape(b, s, H_A, DV_A), iga + lw['b_ig'], fga + lw['b_fg'],
                            mc, mn, mm)
    out_a = _head_rmsnorm(ha, lw['norm_a_g']) * jax.nn.sigmoid(oa.astype(f32))
    xb_c, rconv = _causal_conv(xb, rconv, lw['conv_b_w'], lw['conv_b_b'])
    hb, rh = _rglru(xb_c, rh, lw['w_rg_a'], lw['b_rg_a'], lw['w_rg_x'], lw['b_rg_x'],
                    lw['lru_lambda'])
    out_b = hb * jax.nn.gelu(yb.astype(f32))
    xbc_c, sconv = _causal_conv(xbc, sconv, lw['conv_c_w'], lw['conv_c_b'])
    xbc_c = jax.nn.silu(xbc_c)
    xc, bc, cc = _split(xbc_c, (W_C, G_C * N_C, G_C * N_C))
    dt = jax.nn.softplus(dtc.astype(f32) + lw['dt_bias'])
    A = -jnp.exp(lw['a_log'].astype(f32))
    xc = xc.reshape(b, s, H_C, P_C)
    yc, sh = _ssd(xc, dt, A, bc.reshape(b, s, G_C, N_C), cc.reshape(b, s, G_C, N_C), sh)
    yc = (yc + lw['d_skip'][:, None] * xc).reshape(b, s, W_C)
    out_c = _rmsnorm(yc * jax.nn.silu(zc.astype(f32)), lw['norm_c_g'])
    lg = jax.nn.log_sigmoid((lrd @ lw['w_gla_lr'] + lw['b_gla_lr']).astype(f32)) / GLA_TAU
    od, gs = _gla(qd.reshape(b, s, H_D, DK_D), kd.reshape(b, s, H_D, DK_D),
                  vd.reshape(b, s, H_D, DV_D), lg.reshape(b, s, H_D, DK_D), gs)
    out_d = _head_rmsnorm(od, lw['norm_d_g']) * jax.nn.silu(gd.astype(f32))
    mix = jnp.concatenate([out_a, out_b, out_c, out_d], axis=-1).astype(h.dtype) @ lw['w_out']
    h = h + mix
    hn = _rmsnorm(h, lw['norm_ffn_g'])
    h = h + (_moe(hn, *ffn_w) if use_moe else _swiglu(hn, *ffn_w))
    gate = jax.nn.sigmoid(_rmsnorm(h, lw['norm_ple_g']) @ lw['w_ple_gate'])
    h = h + gate * (p_i @ lw['w_ple_up'])
    return h, (mc, mn, mm, rh, rconv, sh, sconv, gs)


def _trunk(x, p, states, stacked, dense_w, moe_w, final_g):
    b = x.shape[0]
    h = x
    new = [[] for _ in range(N_STATE)]
    for i in range(DEPTH):
        st = _zero_state(b) if states is None else tuple(t[i] for t in states)
        lw = {name: arr[i] for name, arr in stacked.items()}
        ffn_w = tuple(w[i // 2] for w in (moe_w if i % 2 == 1 else dense_w))
        h, st = _layer(h, p[i], st, lw, ffn_w, i % 2 == 1)
        for j in range(N_STATE):
            new[j].append(st[j])
    return _rmsnorm(h, final_g), tuple(jnp.stack(t) for t in new)


def setup_inputs(seed: int = 0) -> dict:
    key = jax.random.key(seed)
    ks = iter(jax.random.split(key, 64))
    f32 = jnp.float32

    def nrm(shape, scale=1.0):
        return jax.random.normal(next(ks), shape, f32) * scale

    def gain(shape):
        return 1.0 + nrm(shape, 0.02)

    n_dense = (DEPTH + 1) // 2
    n_moe = DEPTH // 2
    d = D_MODEL
    u = jax.random.uniform(next(ks), (DEPTH, W_B), f32, 0.9, 0.999)
    a_base = u ** (1.0 / LRU_C)
    dt0 = jnp.exp(jax.random.uniform(next(ks), (DEPTH, H_C), f32, math.log(1e-3), math.log(1e-1)))
    return {
        'x_prompt': nrm((BATCH, SEQ, d)),
        'x_sample': nrm((DEC_BATCH, DEC_SEQ, d)),
        'state_mlstm_c': nrm((DEPTH, DEC_BATCH, H_A, DK_A, DV_A), 0.1),
        'state_mlstm_n': jnp.abs(nrm((DEPTH, DEC_BATCH, H_A, DK_A), 0.1)),
        'state_mlstm_m': nrm((DEPTH, DEC_BATCH, H_A)),
        'state_rglru_h': nrm((DEPTH, DEC_BATCH, W_B), 0.5),
        'state_rglru_conv': nrm((DEPTH, DEC_BATCH, CONV_W - 1, W_B)),
        'state_ssd_h': nrm((DEPTH, DEC_BATCH, H_C, P_C, N_C), 0.1),
        'state_ssd_conv': nrm((DEPTH, DEC_BATCH, CONV_W - 1, XBC_C)),
        'state_gla_s': nrm((DEPTH, DEC_BATCH, H_D, DK_D, DV_D), 0.1),
        'p_prompt': nrm((DEPTH, BATCH, SEQ, PLE_DIM)),
        'p_sample': nrm((DEPTH, DEC_BATCH, DEC_SEQ, PLE_DIM)),
        'norm_mix_g': gain((DEPTH, d)),
        'w_in': nrm((DEPTH, d, IN_W), d ** -0.5),
        'b_ig': nrm((DEPTH, H_A), 0.1) - 1.0,
        'b_fg': jnp.linspace(3.0, 6.0, H_A, dtype=f32)[None] + nrm((DEPTH, H_A), 0.1),
        'norm_a_g': gain((DEPTH, W_A)),
        'conv_b_w': nrm((DEPTH, CONV_W, W_B), CONV_W ** -0.5),
        'conv_b_b': nrm((DEPTH, W_B), 0.02),
        'w_rg_a': nrm((DEPTH, NB_B, BW_B, BW_B), BW_B ** -0.5),
        'b_rg_a': nrm((DEPTH, W_B), 0.02),
        'w_rg_x': nrm((DEPTH, NB_B, BW_B, BW_B), BW_B ** -0.5),
        'b_rg_x': nrm((DEPTH, W_B), 0.02),
        'lru_lambda': jnp.log(a_base) - jnp.log1p(-a_base),
        'conv_c_w': nrm((DEPTH, CONV_W, XBC_C), CONV_W ** -0.5),
        'conv_c_b': nrm((DEPTH, XBC_C), 0.02),
        'dt_bias': dt0 + jnp.log(-jnp.expm1(-dt0)),
        'a_log': jnp.log(jax.random.uniform(next(ks), (DEPTH, H_C), f32, 1.0, 16.0)),
        'd_skip': gain((DEPTH, H_C)),
        'norm_c_g': gain((DEPTH, W_C)),
        'w_gla_lr': nrm((DEPTH, R_D, H_D * DK_D), R_D ** -0.5),
        'b_gla_lr': nrm((DEPTH, H_D * DK_D), 0.02),
        'norm_d_g': gain((DEPTH, W_D)),
        'w_out': nrm((DEPTH, W_MIX, d), W_MIX ** -0.5),
        'norm_ffn_g': gain((DEPTH, d)),
        'w_gate_dense': nrm((n_dense, d, D_FF), d ** -0.5),
        'w_up_dense': nrm((n_dense, d, D_FF), d ** -0.5),
        'w_down_dense': nrm((n_dense, D_FF, d), D_FF ** -0.5),
        'w_router': nrm((n_moe, d, N_EXP), d ** -0.5),
        'w_gate_moe': nrm((n_moe, N_EXP, d, D_FF_EXP), d ** -0.5),
        'w_up_moe': nrm((n_moe, N_EXP, d, D_FF_EXP), d ** -0.5),
        'w_down_moe': nrm((n_moe, N_EXP, D_FF_EXP, d), D_FF_EXP ** -0.5),
        'norm_ple_g': gain((DEPTH, d)),
        'w_ple_gate': nrm((DEPTH, d, d), d ** -0.5),
        'w_ple_up': nrm((DEPTH, PLE_DIM, d), PLE_DIM ** -0.5),
        'final_norm_g': gain((d,)),
    }


def reference(x_prompt, x_sample, state_mlstm_c, state_mlstm_n, state_mlstm_m, state_rglru_h,
              state_rglru_conv, state_ssd_h, state_ssd_conv, state_gla_s, p_prompt, p_sample,
              norm_mix_g, w_in, b_ig, b_fg, norm_a_g, conv_b_w, conv_b_b, w_rg_a, b_rg_a,
              w_rg_x, b_rg_x, lru_lambda, conv_c_w, conv_c_b, dt_bias, a_log, d_skip, norm_c_g,
              w_gla_lr, b_gla_lr, norm_d_g, w_out, norm_ffn_g, w_gate_dense, w_up_dense,
              w_down_dense, w_router, w_gate_moe, w_up_moe, w_down_moe, norm_ple_g, w_ple_gate,
              w_ple_up, final_norm_g):
    stacked = dict(norm_mix_g=norm_mix_g, w_in=w_in, b_ig=b_ig, b_fg=b_fg, norm_a_g=norm_a_g,
                   conv_b_w=conv_b_w, conv_b_b=conv_b_b, w_rg_a=w_rg_a, b_rg_a=b_rg_a,
                   w_rg_x=w_rg_x, b_rg_x=b_rg_x, lru_lambda=lru_lambda, conv_c_w=conv_c_w,
                   conv_c_b=conv_c_b, dt_bias=dt_bias, a_log=a_log, d_skip=d_skip,
                   norm_c_g=norm_c_g, w_gla_lr=w_gla_lr, b_gla_lr=b_gla_lr, norm_d_g=norm_d_g,
                   w_out=w_out, norm_ffn_g=norm_ffn_g, norm_ple_g=norm_ple_g,
                   w_ple_gate=w_ple_gate, w_ple_up=w_ple_up)
    dense_w = (w_gate_dense, w_up_dense, w_down_dense)
    moe_w = (w_router, w_gate_moe, w_up_moe, w_down_moe)
    y_prompt, ps = _trunk(x_prompt, p_prompt, None, stacked, dense_w, moe_w, final_norm_g)
    sample_states = (state_mlstm_c, state_mlstm_n, state_mlstm_m, state_rglru_h,
                     state_rglru_conv, state_ssd_h, state_ssd_conv, state_gla_s)
    y_sample, ss = _trunk(x_sample, p_sample, sample_states, stacked, dense_w, moe_w, final_norm_g)
    (p_mc, p_mn, p_mm, p_rh, p_rconv, p_sh, p_sconv, p_gs) = ps
    (s_mc, s_mn, s_mm, s_rh, s_rconv, s_sh, s_sconv, s_gs) = ss
    return (y_prompt, y_sample, p_mc, p_mn, p_mm, p_rh, p_rconv, p_sh, p_sconv, p_gs,
            s_mc, s_mn, s_mm, s_rh, s_rconv, s_sh, s_sconv, s_gs)
```

```python
import functools
import math

import jax
import jax.numpy as jnp
from jax import lax
from jax.experimental import pallas as pl
from jax.experimental.pallas import tpu as pltpu

F32 = jnp.float32
BF16 = jnp.bfloat16
HI = lax.Precision.HIGHEST

EPS = 1e-6
NB = 4
CONV_W = 4
H_A, DK_A, DV_A = 4, 128, 128
W_B, NB_B = 512, 8
H_C, P_C, N_C = 8, 64, 128
W_C = H_C * P_C
XBC_C = W_C + 2 * N_C
H_D, DK_D, DV_D = 4, 64, 128
R_D = 16
GLA_TAU = 16.0
LRU_C = 8.0
N_EXP = 8
NEG = -1e30

COL_QA, COL_KA, COL_VA, COL_OA = 0, 512, 1024, 1536
COL_XB, COL_YB, COL_ZC = 2048, 2560, 3072
COL_VD, COL_GD = 3584, 4096
COL_XBC = 4608
COL_QD, COL_KD = 5376, 5632
COL_SMALL = 5888
N_PROJ = 6144
VMEM_LIMIT = 56 * 1024 * 1024


def _cparams(n_axes):
    return pltpu.CompilerParams(dimension_semantics=("arbitrary",) * n_axes,
                                vmem_limit_bytes=VMEM_LIMIT)


def _dotb(a, b, dims=(((1,), (0,)), ((), ()))):
    return lax.dot_general(a.astype(BF16), b.astype(BF16), dims, preferred_element_type=F32)


def _doth(a, b, dims=(((1,), (0,)), ((), ()))):
    return lax.dot_general(a, b, dims, precision=HI, preferred_element_type=F32)


NT = (((1,), (1,)), ((), ()))
TN = (((0,), (0,)), ((), ()))


def _log_sigmoid(x):
    return jnp.minimum(x, 0.0) - jnp.log1p(jnp.exp(-jnp.abs(x)))


def _softplus(x):
    return jnp.maximum(x, 0.0) + jnp.log1p(jnp.exp(-jnp.abs(x)))


def _sigmoid(x):
    return 1.0 / (1.0 + jnp.exp(-x))


def _silu(x):
    return x * _sigmoid(x)


def _gelu_tanh(x):
    return 0.5 * x * (1.0 + jnp.tanh(math.sqrt(2.0 / math.pi) * (x + 0.044715 * (x * x * x))))


def _rms_rows(x, g):
    return x * lax.rsqrt(jnp.mean(x * x, axis=-1, keepdims=True) + EPS) * g


def _pick(n, cands):
    for c in cands:
        if n % c == 0:
            return c
    raise ValueError(f"no tile for {n}")


def _rmsnorm_kernel(x_ref, g_ref, o_ref):
    o_ref[...] = _rms_rows(x_ref[...], g_ref[...]).astype(o_ref.dtype)


def _rmsnorm(x, g, out_dtype):
    t, d = x.shape
    tm = _pick(t, (640, 320, 160, 80, 16))
    return pl.pallas_call(
        _rmsnorm_kernel,
        grid=(t // tm,),
        in_specs=[pl.BlockSpec((tm, d), lambda i: (i, 0)), pl.BlockSpec((1, d), lambda i: (0, 0))],
        out_specs=pl.BlockSpec((tm, d), lambda i: (i, 0)),
        out_shape=jax.ShapeDtypeStruct((t, d), out_dtype),
        compiler_params=_cparams(1),
        name="rmsnorm",
    )(x, g.reshape(1, d))


def _matmul_kernel(x_ref, w_ref, o_ref):
    o_ref[...] = jnp.dot(x_ref[...], w_ref[...], preferred_element_type=F32)


def _in_proj(hn, w):
    t, k = hn.shape
    n = w.shape[1]
    tm = _pick(t, (640, 320, 160, 80, 16))
    tn = _pick(n, (1024, 512, 128))
    return pl.pallas_call(
        _matmul_kernel,
        grid=(n // tn, t // tm),
        in_specs=[pl.BlockSpec((tm, k), lambda j, i: (i, 0)), pl.BlockSpec((k, tn), lambda j, i: (0, j))],
        out_specs=pl.BlockSpec((tm, tn), lambda j, i: (i, j)),
        out_shape=jax.ShapeDtypeStruct((t, n), F32),
        compiler_params=_cparams(2),
        name="in_proj",
    )(hn, w)


def _top2_comb(logits):
    lane = lax.broadcasted_iota(jnp.int32, logits.shape, 1)
    v1 = jnp.max(logits, axis=1, keepdims=True)
    i1 = jnp.min(jnp.where(logits == v1, lane, 128), axis=1, keepdims=True)
    l2 = jnp.where(lane == i1, NEG, logits)
    v2 = jnp.max(l2, axis=1, keepdims=True)
    i2 = jnp.min(jnp.where(l2 == v2, lane, 128), axis=1, keepdims=True)
    e2 = jnp.exp(v2 - v1)
    g1 = 1.0 / (1.0 + e2)
    g2 = e2 / (1.0 + e2)
    return jnp.where(lane == i1, g1, 0.0) + jnp.where(lane == i2, g2, 0.0)


def _out_proj_kernel(*refs, router):
    ma, mb, mc, md, wa, wb, wc, wd, h_ref, g_ref = refs[:10]
    if router:
        wr_ref = refs[10]
        h_out, hn_out, comb_out = refs[11:14]
    else:
        h_out, hn_out = refs[10:12]
    acc = jnp.dot(ma[...], wa[...], preferred_element_type=F32)
    acc += jnp.dot(mb[...], wb[...], preferred_element_type=F32)
    acc += jnp.dot(mc[...], wc[...], preferred_element_type=F32)
    acc += jnp.dot(md[...], wd[...], preferred_element_type=F32)
    h = h_ref[...] + acc
    h_out[...] = h
    hn = _rms_rows(h, g_ref[...])
    hn_out[...] = hn.astype(BF16)
    if router:
        logits = _doth(hn, wr_ref[...])
        lane = lax.broadcasted_iota(jnp.int32, logits.shape, 1)
        comb_out[...] = _top2_comb(jnp.where(lane < N_EXP, logits, NEG))


def _out_proj(mixes, w_out, h, g, w_router):
    t, d = h.shape
    tm = _pick(t, (320, 160, 80, 16))
    router = w_router is not None
    wq = w_out.shape[0] // 4
    in_specs = [pl.BlockSpec((tm, wq), lambda i: (i, 0)) for _ in range(4)]
    in_specs += [pl.BlockSpec((wq, d), lambda i, k=k: (k, 0)) for k in range(4)]
    in_specs += [pl.BlockSpec((tm, d), lambda i: (i, 0)), pl.BlockSpec((1, d), lambda i: (0, 0))]
    args = list(mixes) + [w_out] * 4 + [h, g.reshape(1, d)]
    out_specs = [pl.BlockSpec((tm, d), lambda i: (i, 0)), pl.BlockSpec((tm, d), lambda i: (i, 0))]
    out_shape = [jax.ShapeDtypeStruct((t, d), F32), jax.ShapeDtypeStruct((t, d), BF16)]
    if router:
        in_specs.append(pl.BlockSpec((d, 128), lambda i: (0, 0)))
        args.append(w_router)
        out_specs.append(pl.BlockSpec((tm, 128), lambda i: (i, 0)))
        out_shape.append(jax.ShapeDtypeStruct((t, 128), F32))
    return pl.pallas_call(
        functools.partial(_out_proj_kernel, router=router),
        grid=(t // tm,),
        in_specs=in_specs, out_specs=out_specs, out_shape=out_shape,
        compiler_params=_cparams(1),
        name="out_proj",
    )(*args)


def _ffn_kernel(*refs, n_exp):
    if n_exp > 1:
        x_ref, wg_ref, wu_ref, wd_ref, h_ref, g_ref, comb_ref, h_out, hn_out, acc_ref = refs
    else:
        x_ref, wg_ref, wu_ref, wd_ref, h_ref, g_ref, h_out, hn_out, acc_ref = refs
    e = pl.program_id(1)
    f = pl.program_id(2)
    first = jnp.logical_and(e == 0, f == 0)
    last = jnp.logical_and(e == pl.num_programs(1) - 1, f == pl.num_programs(2) - 1)

    @pl.when(first)
    def _():
        acc_ref[...] = jnp.zeros_like(acc_ref)

    x = x_ref[...]
    gate = jnp.dot(x, wg_ref[...], preferred_element_type=F32)
    up = jnp.dot(x, wu_ref[...], preferred_element_type=F32)
    hid = _silu(gate) * up
    if n_exp > 1:
        comb = comb_ref[...]
        lane = lax.broadcasted_iota(jnp.int32, comb.shape, 1)
        hid = hid * jnp.sum(jnp.where(lane == e, comb, 0.0), axis=1, keepdims=True)
    acc_ref[...] += jnp.dot(hid.astype(BF16), wd_ref[...], preferred_element_type=F32)

    @pl.when(last)
    def _():
        h = h_ref[...] + acc_ref[...]
        h_out[...] = h
        hn_out[...] = _rms_rows(h, g_ref[...]).astype(BF16)


def _ffn(x, wg, wu, wd, h, g, comb):
    t, d = x.shape
    n_exp, _, ff = wg.shape
    tm = _pick(t, (640, 320, 160, 80, 16))
    tf = _pick(ff, (512, 256, 128))
    in_specs = [
        pl.BlockSpec((tm, d), lambda i, e, f: (i, 0)),
        pl.BlockSpec((None, d, tf), lambda i, e, f: (e, 0, f)),
        pl.BlockSpec((None, d, tf), lambda i, e, f: (e, 0, f)),
        pl.BlockSpec((None, tf, d), lambda i, e, f: (e, f, 0)),
        pl.BlockSpec((tm, d), lambda i, e, f: (i, 0)),
        pl.BlockSpec((1, d), lambda i, e, f: (0, 0)),
    ]
    args = [x, wg, wu, wd, h, g.reshape(1, d)]
    if n_exp > 1:
        in_specs.append(pl.BlockSpec((tm, 128), lambda i, e, f: (i, 0)))
        args.append(comb)
    return pl.pallas_call(
        functools.partial(_ffn_kernel, n_exp=n_exp),
        grid=(t // tm, n_exp, ff // tf),
        in_specs=in_specs,
        out_specs=[pl.BlockSpec((tm, d), lambda i, e, f: (i, 0)), pl.BlockSpec((tm, d), lambda i, e, f: (i, 0))],
        out_shape=[jax.ShapeDtypeStruct((t, d), F32), jax.ShapeDtypeStruct((t, d), BF16)],
        scratch_shapes=[pltpu.VMEM((tm, d), F32)],
        compiler_params=_cparams(3),
        name="ffn",
    )(*args)


def _ple_kernel(hn_ref, p_ref, h_ref, wg_ref, wu_ref, g_ref, h_out, n_out):
    gate = _sigmoid(jnp.dot(hn_ref[...], wg_ref[...], preferred_element_type=F32))
    up = jnp.dot(p_ref[...], wu_ref[...], preferred_element_type=F32)
    h = h_ref[...] + gate * up
    h_out[...] = h
    n_out[...] = _rms_rows(h, g_ref[...]).astype(n_out.dtype)


def _ple(hn, p, h, w_gate, w_up, g_next, next_dtype):
    t, d = h.shape
    pd = p.shape[1]
    tm = _pick(t, (320, 160, 80, 16))
    return pl.pallas_call(
        _ple_kernel,
        grid=(t // tm,),
        in_specs=[pl.BlockSpec((tm, d), lambda i: (i, 0)), pl.BlockSpec((tm, pd), lambda i: (i, 0)),
                  pl.BlockSpec((tm, d), lambda i: (i, 0)), pl.BlockSpec((d, d), lambda i: (0, 0)),
                  pl.BlockSpec((pd, d), lambda i: (0, 0)), pl.BlockSpec((1, d), lambda i: (0, 0))],
        out_specs=[pl.BlockSpec((tm, d), lambda i: (i, 0)), pl.BlockSpec((tm, d), lambda i: (i, 0))],
        out_shape=[jax.ShapeDtypeStruct((t, d), F32), jax.ShapeDtypeStruct((t, d), next_dtype)],
        compiler_params=_cparams(1),
        name="ple",
    )(hn, p, h, w_gate, w_up, g_next.reshape(1, d))


def _tri(n):
    r = lax.broadcasted_iota(jnp.int32, (n, n), 0)
    c = lax.broadcasted_iota(jnp.int32, (n, n), 1)
    return c <= r


def _prompt_specs(nc, cols, L, with_small_t):
    specs = []
    for b in range(NB):
        for (col, width) in cols:
            specs.append(pl.BlockSpec((L, width), lambda c, b=b, j=col // width: (b * nc + c, j)))
        if with_small_t:
            specs.append(pl.BlockSpec((32, L), lambda c, b=b: (0, b * nc + c)))
    return specs


def _mlstm_prompt_kernel(*refs, L):
    ins = refs[:6 * NB]
    bias_row_ref, bias_col_ref, g_ref = refs[6 * NB:6 * NB + 3]
    outs = refs[6 * NB + 3:6 * NB + 3 + NB]
    c_ref, n_ref, m_ref = refs[6 * NB + 3 + NB:]

    @pl.when(pl.program_id(0) == 0)
    def _():
        c_ref[...] = jnp.zeros_like(c_ref)
        n_ref[...] = jnp.zeros_like(n_ref)
        m_ref[...] = jnp.zeros_like(m_ref)

    causal = _tri(L)
    tril = causal.astype(F32)
    triu = (lax.broadcasted_iota(jnp.int32, (L, L), 0) <= lax.broadcasted_iota(jnp.int32, (L, L), 1)).astype(F32)
    g = g_ref[...]
    for b in range(NB):
        q_ref, k_ref, v_ref, o_ref, sm_ref, smt_ref = ins[6 * b:6 * b + 6]
        sm = sm_ref[...] + bias_row_ref[...]
        f_col = _doth(tril, _log_sigmoid(sm))
        smt = smt_ref[...] + bias_col_ref[...]
        f_row = _doth(_log_sigmoid(smt), triu)
        for h in range(H_A):
            hs = slice(h * DK_A, (h + 1) * DK_A)
            r = b * H_A + h
            ig_c = sm[:, h:h + 1]
            fc = f_col[:, 4 + h:5 + h]
            a_row = smt[h:h + 1, :] - f_row[4 + h:5 + h, :]
            m_prev = m_ref[r:r + 1, 0:1]
            cm = jnp.max(jnp.where(causal, a_row, NEG), axis=1, keepdims=True)
            mt = fc + jnp.maximum(m_prev, cm)
            dmat = jnp.exp(jnp.where(causal, (fc - mt) + a_row, NEG))
            qs = q_ref[:, hs] * (DK_A ** -0.5)
            kh = k_ref[:, hs]
            vh = v_ref[:, hs]
            sc = _dotb(qs, kh, NT) * dmat
            inter = jnp.exp(fc + m_prev - mt)
            c_old = c_ref[b, h]
            n_old = n_ref[r:r + 1, :]
            num = _dotb(sc, vh) + inter * _dotb(qs, c_old)
            den = jnp.sum(sc, axis=1, keepdims=True) + inter * jnp.sum(qs * n_old, axis=1, keepdims=True)
            den = jnp.maximum(jnp.abs(den), jnp.exp(-mt))
            hc = num / den
            m_last = mt[L - 1:L, :]
            f_last = fc[L - 1:L, :]
            wk = jnp.exp(f_last - fc + ig_c - m_last)
            d0 = jnp.exp(f_last + m_prev - m_last)
            kw = kh * wk
            c_ref[b, h] = d0 * c_old + _dotb(kw, vh, TN)
            n_ref[r:r + 1, :] = d0 * n_old + jnp.sum(kw, axis=0, keepdims=True)
            m_ref[r:r + 1, :] = jnp.broadcast_to(m_last, (1, 128))
            y = _rms_rows(hc, g[:, hs]) * _sigmoid(o_ref[:, hs])
            outs[b][:, hs] = y.astype(BF16)


def _mlstm_prompt(proj, small_t, s_len, b_ig, b_fg, g, L):
    nc = s_len // L
    cols = [(COL_QA, 512), (COL_KA, 512), (COL_VA, 512), (COL_OA, 512), (COL_SMALL, 128)]
    bias_row = jnp.zeros((1, 128), F32).at[0, 0:4].set(b_ig).at[0, 4:8].set(b_fg)
    bias_col = jnp.zeros((32, 1), F32).at[0:4, 0].set(b_ig).at[4:8, 0].set(b_fg)
    in_specs = _prompt_specs(nc, cols, L, True)
    args = []
    for b in range(NB):
        args += [proj] * 5 + [small_t]
    in_specs += [pl.BlockSpec((1, 128), lambda c: (0, 0)), pl.BlockSpec((32, 1), lambda c: (0, 0)),
                 pl.BlockSpec((1, 512), lambda c: (0, 0))]
    args += [bias_row, bias_col, g.reshape(1, 512)]
    out_specs = [pl.BlockSpec((L, 512), lambda c: (c, 0)) for _ in range(NB)]
    out_shape = [jax.ShapeDtypeStruct((s_len, 512), BF16) for _ in range(NB)]
    out_specs += [pl.BlockSpec((NB, H_A, DK_A, DV_A), lambda c: (0, 0, 0, 0)),
                  pl.BlockSpec((NB * H_A, 128), lambda c: (0, 0)),
                  pl.BlockSpec((NB * H_A, 128), lambda c: (0, 0))]
    out_shape += [jax.ShapeDtypeStruct((NB, H_A, DK_A, DV_A), F32),
                  jax.ShapeDtypeStruct((NB * H_A, 128), F32),
                  jax.ShapeDtypeStruct((NB * H_A, 128), F32)]
    res = pl.pallas_call(
        functools.partial(_mlstm_prompt_kernel, L=L),
        grid=(nc,), in_specs=in_specs, out_specs=out_specs, out_shape=out_shape,
        compiler_params=_cparams(1), name="mlstm_prompt",
    )(*args)
    outs, (c, n, m) = res[:NB], res[NB:]
    return outs, c, n.reshape(NB, H_A, DK_A), m[:, 0].reshape(NB, H_A)


def _conv_chunk(xpad_ref, tail_ref, x, w, bias, L):
    xpad_ref[0:8, :] = tail_ref[...]
    xpad_ref[8:8 + L, :] = x
    y = bias + xpad_ref[5:5 + L, :] * w[0:1, :]
    for k in range(1, CONV_W):
        y = y + xpad_ref[5 + k:5 + k + L, :] * w[k:k + 1, :]
    tail_ref[...] = x[L - 8:L, :]
    return y


def _ssd_prompt_kernel(*refs, L):
    ins = refs[:4 * NB]
    (bias_row_ref, bias_col_ref, a_row_ref, a_col_ref, cw_ref, cb_ref, edt_ref, dskip_ref,
     g_ref) = refs[4 * NB:4 * NB + 9]
    outs = refs[4 * NB + 9:4 * NB + 9 + NB]
    h_ref, conv_ref = refs[4 * NB + 9 + NB:4 * NB + 11 + NB]
    xpad_ref, tail_ref, y_ref = refs[4 * NB + 11 + NB:]
    last = pl.program_id(0) == pl.num_programs(0) - 1

    @pl.when(pl.program_id(0) == 0)
    def _():
        h_ref[...] = jnp.zeros_like(h_ref)
        tail_ref[...] = jnp.zeros_like(tail_ref)

    causal = _tri(L)
    tril = causal.astype(F32)
    triu = (lax.broadcasted_iota(jnp.int32, (L, L), 0) <= lax.broadcasted_iota(jnp.int32, (L, L), 1)).astype(F32)
    for b in range(NB):
        z_ref, xbc_ref, sm_ref, smt_ref = ins[4 * b:4 * b + 4]
        xbc = xbc_ref[...]
        conv = _silu(_conv_chunk(xpad_ref, tail_ref.at[b], xbc, cw_ref[...], cb_ref[...], L))

        @pl.when(last)
        def _():
            conv_ref[b] = xbc[L - 3:L, :]

        xc = conv[:, 0:W_C]
        bm = conv[:, W_C:W_C + N_C]
        cm = conv[:, W_C + N_C:XBC_C]
        dt_c = _softplus(sm_ref[...] + bias_row_ref[...])
        cum_c = _doth(tril, dt_c * a_row_ref[...])
        dt_r = _softplus(smt_ref[...] + bias_col_ref[...])
        cum_r = _doth(dt_r * a_col_ref[...], triu)
        xdt = xc * _doth(dt_c, edt_ref[...])
        cb_mat = _dotb(cm, bm, NT)
        for h in range(H_C):
            ps = slice(h * P_C, (h + 1) * P_C)
            cc = cum_c[:, 8 + h:9 + h]
            cr = cum_r[8 + h:9 + h, :]
            dec = jnp.exp(jnp.where(causal, cc - cr, NEG))
            h_old = h_ref[b, h]
            xh = xdt[:, ps]
            y_ref[:, ps] = _dotb(cb_mat * dec, xh) + _dotb(cm, h_old, NT) * jnp.exp(cc)
            c_last = cc[L - 1:L, :]
            wl = jnp.exp(c_last - cc)
            h_ref[b, h] = jnp.exp(c_last) * h_old + _dotb(xh * wl, bm, TN)
        yc = (y_ref[...] + dskip_ref[...] * xc) * _silu(z_ref[...])
        outs[b][...] = _rms_rows(yc, g_ref[...]).astype(BF16)


def _ssd_prompt(proj, small_t, s_len, lw, L):
    nc = s_len // L
    cols = [(COL_ZC, 512), (COL_XBC, 768), (COL_SMALL, 128)]
    in_specs = _prompt_specs(nc, cols, L, True)
    args = []
    for b in range(NB):
        args += [proj] * 3 + [small_t]
    a = -jnp.exp(lw['a_log'])
    bias_row = jnp.zeros((1, 128), F32).at[0, 8:16].set(lw['dt_bias'])
    bias_col = jnp.zeros((32, 1), F32).at[8:16, 0].set(lw['dt_bias'])
    a_row = jnp.zeros((1, 128), F32).at[0, 8:16].set(a)
    a_col = jnp.zeros((32, 1), F32).at[8:16, 0].set(a)
    consts = [bias_row, bias_col, a_row, a_col, lw['conv_c_w'], lw['conv_c_b'].reshape(1, XBC_C),
              _expand_mat(8, H_C, P_C), jnp.repeat(lw['d_skip'], P_C).reshape(1, W_C),
              lw['norm_c_g'].reshape(1, W_C)]
    for cst in consts:
        in_specs.append(pl.BlockSpec(cst.shape, lambda c: (0, 0)))
    args += consts
    out_specs = [pl.BlockSpec((L, 512), lambda c: (c, 0)) for _ in range(NB)]
    out_shape = [jax.ShapeDtypeStruct((s_len, 512), BF16) for _ in range(NB)]
    out_specs += [pl.BlockSpec((NB, H_C, P_C, N_C), lambda c: (0, 0, 0, 0)),
                  pl.BlockSpec((NB, CONV_W - 1, XBC_C), lambda c: (0, 0, 0))]
    out_shape += [jax.ShapeDtypeStruct((NB, H_C, P_C, N_C), F32),
                  jax.ShapeDtypeStruct((NB, CONV_W - 1, XBC_C), F32)]
    res = pl.pallas_call(
        functools.partial(_ssd_prompt_kernel, L=L),
        grid=(nc,), in_specs=in_specs, out_specs=out_specs, out_shape=out_shape,
        scratch_shapes=[pltpu.VMEM((L + 8, XBC_C), F32), pltpu.VMEM((NB, 8, XBC_C), F32),
                        pltpu.VMEM((L, W_C), F32)],
        compiler_params=_cparams(1), name="ssd_prompt",
    )(*args)
    return res[:NB], res[NB], res[NB + 1]


def _expand_mat(lane0, n_heads, width, rows=128):
    r = jnp.arange(rows)[:, None]
    c = jnp.arange(n_heads * width)[None, :]
    return (r == lane0 + c // width).astype(F32)


def _block_ones(n_heads, w_in, w_out):
    r = jnp.arange(n_heads * w_in)[:, None] // w_in
    c = jnp.arange(n_heads * w_out)[None, :] // w_out
    return (r == c).astype(F32)


GLA_SUB = 16


def _gla_prompt_kernel(*refs, L):
    ins = refs[:5 * NB]
    wlr_ref, blr_ref, seg_ref, g_ref = refs[5 * NB:5 * NB + 4]
    outs = refs[5 * NB + 4:5 * NB + 4 + NB]
    s_ref = refs[5 * NB + 4 + NB]

    @pl.when(pl.program_id(0) == 0)
    def _():
        s_ref[...] = jnp.zeros_like(s_ref)

    nsub = L // GLA_SUB
    tril = _tri(L).astype(F32)
    row = lax.broadcasted_iota(jnp.int32, (L, 1), 0)
    tl = row % GLA_SUB
    ones_l = jnp.ones((L, 128), F32)
    seg = seg_ref[...]

    def group_rows(x, j):
        return jnp.concatenate(
            [jnp.broadcast_to(x[i * GLA_SUB + j:i * GLA_SUB + j + 1, :], (GLA_SUB, x.shape[1]))
             for i in range(nsub)], axis=0)

    for b in range(NB):
        q_ref, k_ref, v_ref, gd_ref, sm_ref = ins[5 * b:5 * b + 5]
        lg = _log_sigmoid(_doth(sm_ref[...], wlr_ref[...]) + blr_ref[...]) / GLA_TAU
        bc = _doth(tril, lg)
        qs = q_ref[...] * (DK_D ** -0.5)
        k = k_ref[...]
        v = v_ref[...]
        kt = k * jnp.exp(group_rows(bc, GLA_SUB - 1) - bc)
        att = [None] * H_D
        for j in range(nsub - 1):
            r_j = bc[j * GLA_SUB + GLA_SUB - 1:j * GLA_SUB + GLA_SUB, :]
            qj = jnp.where(row >= (j + 1) * GLA_SUB, qs * jnp.exp(jnp.minimum(bc - r_j, 0.0)), 0.0)
            kj = jnp.where(jnp.logical_and(row >= j * GLA_SUB, row < (j + 1) * GLA_SUB), kt, 0.0)
            for h in range(H_D):
                ks = slice(h * DK_D, (h + 1) * DK_D)
                a = _dotb(qj[:, ks], kj[:, ks], NT)
                att[h] = a if att[h] is None else att[h] + a
        qe = qs * jnp.exp(bc)
        o = jnp.concatenate(
            [_dotb(att[h], v[:, h * DV_D:(h + 1) * DV_D])
             + _dotb(qe[:, h * DK_D:(h + 1) * DK_D], s_ref[b, h]) for h in range(H_D)], axis=1)
        for j in range(GLA_SUB):
            w = qs * group_rows(k, j) * jnp.exp(jnp.minimum(bc - group_rows(bc, j), 0.0))
            w = jnp.where(tl >= j, w, 0.0)
            o = o + jnp.dot(w.astype(BF16), seg, preferred_element_type=F32) * group_rows(v, j)
        b_last = bc[L - 1:L, :]
        kl = k * jnp.exp(b_last - bc)
        e_last = jnp.exp(_doth(lg, ones_l, TN))
        for h in range(H_D):
            ks = slice(h * DK_D, (h + 1) * DK_D)
            s_ref[b, h] = e_last[ks, :] * s_ref[b, h] + _dotb(kl[:, ks], v[:, h * DV_D:(h + 1) * DV_D], TN)
        gate = _silu(gd_ref[...])
        g = g_ref[...]
        for h in range(H_D):
            vs = slice(h * DV_D, (h + 1) * DV_D)
            outs[b][:, vs] = (_rms_rows(o[:, vs], g[:, vs]) * gate[:, vs]).astype(BF16)


def _gla_prompt(proj, s_len, lw, L):
    nc = s_len // L
    cols = [(COL_QD, 256), (COL_KD, 256), (COL_VD, 512), (COL_GD, 512), (COL_SMALL, 128)]
    in_specs = _prompt_specs(nc, cols, L, False)
    args = [proj] * (5 * NB)
    wlr = jnp.zeros((128, H_D * DK_D), F32).at[16:32, :].set(lw['w_gla_lr'])
    consts = [wlr, lw['b_gla_lr'].reshape(1, -1), _block_ones(H_D, DK_D, DV_D).astype(BF16),
              lw['norm_d_g'].reshape(1, -1)]
    for cst in consts:
        in_specs.append(pl.BlockSpec(cst.shape, lambda c: (0, 0)))
    args += consts
    out_specs = [pl.BlockSpec((L, 512), lambda c: (c, 0)) for _ in range(NB)]
    out_shape = [jax.ShapeDtypeStruct((s_len, 512), BF16) for _ in range(NB)]
    out_specs.append(pl.BlockSpec((NB, H_D, DK_D, DV_D), lambda c: (0, 0, 0, 0)))
    out_shape.append(jax.ShapeDtypeStruct((NB, H_D, DK_D, DV_D), F32))
    res = pl.pallas_call(
        functools.partial(_gla_prompt_kernel, L=L),
        grid=(nc,), in_specs=in_specs, out_specs=out_specs, out_shape=out_shape,
        compiler_params=_cparams(1), name="gla_prompt",
    )(*args)
    return res[:NB], res[NB]


def _rglru_gates(xc, wa, ba, wx, bx, sp_lam):
    r = _sigmoid(_dotb(xc, wa) + ba)
    i = _sigmoid(_dotb(xc, wx) + bx)
    log_a = -LRU_C * r * sp_lam
    a = jnp.exp(log_a)
    s = jnp.tanh(-log_a)
    u = jnp.sqrt(2.0 * s / (1.0 + s)) * (i * xc)
    return a, u


def _rglru_prompt_kernel(*refs, L):
    ins = refs[:2 * NB]
    cw_ref, cb_ref, wa_ref, ba_ref, wx_ref, bx_ref, lam_ref = refs[2 * NB:2 * NB + 7]
    outs = refs[2 * NB + 7:2 * NB + 7 + NB]
    h_ref, conv_ref = refs[2 * NB + 7 + NB:2 * NB + 9 + NB]
    xpad_ref, tail_ref, a_ref, u_ref, hs_ref = refs[2 * NB + 9 + NB:]
    last = pl.program_id(0) == pl.num_programs(0) - 1

    @pl.when(pl.program_id(0) == 0)
    def _():
        h_ref[...] = jnp.zeros_like(h_ref)
        tail_ref[...] = jnp.zeros_like(tail_ref)

    sp_lam = _softplus(-lam_ref[...])
    for b in range(NB):
        xb = ins[2 * b][...]
        xc = _conv_chunk(xpad_ref, tail_ref.at[b], xb, cw_ref[...], cb_ref[...], L)

        @pl.when(last)
        def _():
            conv_ref[b] = xb[L - 3:L, :]

        a, u = _rglru_gates(xc, wa_ref[...], ba_ref[...], wx_ref[...], bx_ref[...], sp_lam)
        a_ref[b] = a
        u_ref[b] = u

    def step(t, hs):
        new = []
        for b in range(NB):
            hb = a_ref[b, pl.ds(t, 1), :] * hs[b] + u_ref[b, pl.ds(t, 1), :]
            hs_ref[b, pl.ds(t, 1), :] = hb
            new.append(hb)
        return tuple(new)

    hs = lax.fori_loop(0, L, step, tuple(h_ref[b:b + 1, :] for b in range(NB)), unroll=8)
    for b in range(NB):
        h_ref[b:b + 1, :] = hs[b]
        outs[b][...] = (hs_ref[b] * _gelu_tanh(ins[2 * b + 1][...])).astype(BF16)


def _block_diag(w):
    n, bi, bj = w.shape
    return (jnp.eye(n, dtype=w.dtype)[:, None, :, None] * w[:, :, None, :]).reshape(n * bi, n * bj)


def _rglru_consts(lw):
    return [lw['conv_b_w'], lw['conv_b_b'].reshape(1, W_B), _block_diag(lw['w_rg_a']).astype(BF16),
            lw['b_rg_a'].reshape(1, W_B), _block_diag(lw['w_rg_x']).astype(BF16),
            lw['b_rg_x'].reshape(1, W_B), lw['lru_lambda'].reshape(1, W_B)]


def _rglru_prompt(proj, s_len, lw, L):
    nc = s_len // L
    cols = [(COL_XB, 512), (COL_YB, 512)]
    in_specs = _prompt_specs(nc, cols, L, False)
    args = [proj] * (2 * NB)
    consts = _rglru_consts(lw)
    for cst in consts:
        in_specs.append(pl.BlockSpec(cst.shape, lambda c: (0, 0)))
    args += consts
    out_specs = [pl.BlockSpec((L, 512), lambda c: (c, 0)) for _ in range(NB)]
    out_shape = [jax.ShapeDtypeStruct((s_len, 512), BF16) for _ in range(NB)]
    out_specs += [pl.BlockSpec((NB, W_B), lambda c: (0, 0)),
                  pl.BlockSpec((NB, CONV_W - 1, W_B), lambda c: (0, 0, 0))]
    out_shape += [jax.ShapeDtypeStruct((NB, W_B), F32), jax.ShapeDtypeStruct((NB, CONV_W - 1, W_B), F32)]
    res = pl.pallas_call(
        functools.partial(_rglru_prompt_kernel, L=L),
        grid=(nc,), in_specs=in_specs, out_specs=out_specs, out_shape=out_shape,
        scratch_shapes=[pltpu.VMEM((L + 8, W_B), F32), pltpu.VMEM((NB, 8, W_B), F32),
                        pltpu.VMEM((NB, L, W_B), F32), pltpu.VMEM((NB, L, W_B), F32),
                        pltpu.VMEM((NB, L, W_B), F32)],
        compiler_params=_cparams(1), name="rglru_prompt",
    )(*args)
    return res[:NB], res[NB], res[NB + 1]


BT = 16


def _row8(ref, b, sl):
    return jnp.broadcast_to(ref[pl.ds(b, 1), sl], (8, sl.stop - sl.start))


def _first_row(x8):
    rid = lax.broadcasted_iota(jnp.int32, x8.shape, 0)
    return jnp.where(rid == 0, x8, 0.0)


def _mlstm_sample_kernel(q_ref, k_ref, v_ref, o_ref, sm_ref, c_ref, n_ref, m_ref, bias_ref, g_ref,
                         eig_ref, efg_ref, ones_ref, out_ref, c_out, n_out, m_out,
                         qs_scr, kw_scr, d0_scr, qc_scr):
    sm = sm_ref[...] + bias_ref[...]
    ig = _doth(sm, eig_ref[...])
    lf = _doth(_log_sigmoid(sm), efg_ref[...])
    m_old = m_ref[...]
    mt = jnp.maximum(lf + m_old, ig)
    dsc = jnp.exp(ig - mt)
    inter = jnp.exp(lf + m_old - mt)
    qs = q_ref[...] * (DK_A ** -0.5)
    k = k_ref[...]
    v = v_ref[...]
    n_old = n_ref[...]
    ones = ones_ref[...]
    qk = _doth(qs * k, ones)
    qn = _doth(qs * n_old, ones)
    kw = dsc * k
    qs_scr[...] = qs
    kw_scr[...] = kw
    d0_scr[...] = inter

    def body(b, carry):
        for h in range(H_A):
            hs = slice(h * DK_A, (h + 1) * DK_A)
            c_old = c_ref[b, h]
            qc_scr[pl.ds(b, 1), hs] = _doth(_row8(qs_scr, b, hs), c_old)[0:1, :]
            upd = _doth(_first_row(_row8(kw_scr, b, hs)), _row8(v_ref, b, hs), TN)
            c_out[b, h] = jnp.broadcast_to(d0_scr[pl.ds(b, 1), hs], (DK_A, DV_A)) * c_old + upd
        return carry

    for b in range(BT):
        body(b, 0)
    sc = qk * dsc
    num = sc * v + inter * qc_scr[...]
    den = jnp.maximum(jnp.abs(sc + inter * qn), jnp.exp(-mt))
    hc = num / den
    n_out[...] = inter * n_old + kw
    m_out[...] = mt
    ms = _doth(hc * hc, ones) * (1.0 / DV_A)
    out_ref[...] = (hc * lax.rsqrt(ms + EPS) * g_ref[...] * _sigmoid(o_ref[...])).astype(BF16)


def _sample_spec(row0, width, col):
    return pl.BlockSpec((BT, width), lambda i: (row0 // BT + i, col // width))


def _const_spec(x):
    return pl.BlockSpec(x.shape, lambda i: (0,) * x.ndim)


def _mlstm_sample(proj, row0, nbs, c0, n0, m0, b_ig, b_fg, g):
    bias = jnp.zeros((1, 128), F32).at[0, 0:4].set(b_ig).at[0, 4:8].set(b_fg)
    consts = [bias, g.reshape(1, 512), _expand_mat(0, H_A, DK_A), _expand_mat(4, H_A, DK_A),
              _block_ones(H_A, DK_A, DK_A)]
    in_specs = [_sample_spec(row0, 512, COL_QA), _sample_spec(row0, 512, COL_KA),
                _sample_spec(row0, 512, COL_VA), _sample_spec(row0, 512, COL_OA),
                _sample_spec(row0, 128, COL_SMALL),
                pl.BlockSpec((BT, H_A, DK_A, DV_A), lambda i: (i, 0, 0, 0)),
                pl.BlockSpec((BT, 512), lambda i: (i, 0)), pl.BlockSpec((BT, 512), lambda i: (i, 0))]
    in_specs += [_const_spec(x) for x in consts]
    res = pl.pallas_call(
        _mlstm_sample_kernel,
        grid=(nbs // BT,), in_specs=in_specs,
        out_specs=[pl.BlockSpec((BT, 512), lambda i: (i, 0)),
                   pl.BlockSpec((BT, H_A, DK_A, DV_A), lambda i: (i, 0, 0, 0)),
                   pl.BlockSpec((BT, 512), lambda i: (i, 0)), pl.BlockSpec((BT, 512), lambda i: (i, 0))],
        out_shape=[jax.ShapeDtypeStruct((nbs, 512), BF16), jax.ShapeDtypeStruct(c0.shape, F32),
                   jax.ShapeDtypeStruct((nbs, 512), F32), jax.ShapeDtypeStruct((nbs, 512), F32)],
        scratch_shapes=[pltpu.VMEM((BT, 512), F32)] * 4,
        compiler_params=_cparams(1), name="mlstm_sample",
    )(proj, proj, proj, proj, proj, c0, n0.reshape(nbs, 512), jnp.repeat(m0, DK_A, axis=1), *consts)
    out, c1, n1, m1 = res
    return out, c1, n1.reshape(nbs, H_A, DK_A), m1[:, ::DK_A]


def _conv_step(conv_ref, x, w, bias, conv_out):
    y = bias + conv_ref[0] * w[0:1, :] + conv_ref[1] * w[1:2, :] + conv_ref[2] * w[2:3, :] + x * w[3:4, :]
    conv_out[0] = conv_ref[1]
    conv_out[1] = conv_ref[2]
    conv_out[2] = x
    return y


def _ssd_sample_kernel(z_ref, xbc_ref, sm_ref, h_ref, conv_ref, bias_ref, a_ref, cw_ref, cb_ref, e64_ref,
                       e128_ref, dskip_ref, g_ref, out_ref, h_out, conv_out, xdt_scr, b_scr, c_scr, ea_scr,
                       hc_scr):
    conv = _silu(_conv_step(conv_ref, xbc_ref[...], cw_ref[...], cb_ref[...], conv_out))
    xc = conv[:, 0:W_C]
    bm = conv[:, W_C:W_C + N_C]
    cm = conv[:, W_C + N_C:XBC_C]
    dt = _softplus(sm_ref[...] + bias_ref[...])
    ea = jnp.exp(dt * a_ref[...])
    xdt = xc * _doth(dt, e64_ref[...])
    xdt_scr[...] = xdt
    b_scr[...] = bm
    c_scr[...] = cm
    ea_scr[...] = _doth(ea, e128_ref[...])
    full = slice(0, N_C)

    def body(b, carry):
        c8 = _row8(c_scr, b, full)
        b8 = _row8(b_scr, b, full)
        for h in range(H_C):
            ps = slice(h * P_C, (h + 1) * P_C)
            h_old = h_ref[b, h]
            hc_scr[pl.ds(b, 1), ps] = _doth(c8, h_old, NT)[0:1, :]
            upd = _doth(_first_row(_row8(xdt_scr, b, ps)), b8, TN)
            h_out[b, h] = jnp.broadcast_to(ea_scr[pl.ds(b, 1), h * N_C:(h + 1) * N_C], (P_C, N_C)) * h_old + upd
        return carry

    for b in range(BT):
        body(b, 0)
    cb = jnp.sum(cm * bm, axis=1, keepdims=True)
    y = cb * xdt + _doth(ea, e64_ref[...]) * hc_scr[...]
    yc = (y + dskip_ref[...] * xc) * _silu(z_ref[...])
    out_ref[...] = _rms_rows(yc, g_ref[...]).astype(BF16)


def _ssd_sample(proj, row0, nbs, h0, conv0, lw):
    a = -jnp.exp(lw['a_log'])
    consts = [jnp.zeros((1, 128), F32).at[0, 8:16].set(lw['dt_bias']),
              jnp.zeros((1, 128), F32).at[0, 8:16].set(a),
              lw['conv_c_w'], lw['conv_c_b'].reshape(1, XBC_C), _expand_mat(8, H_C, P_C),
              _expand_mat(8, H_C, N_C), jnp.repeat(lw['d_skip'], P_C).reshape(1, W_C),
              lw['norm_c_g'].reshape(1, W_C)]
    in_specs = [_sample_spec(row0, 512, COL_ZC), _sample_spec(row0, 768, COL_XBC),
                _sample_spec(row0, 128, COL_SMALL),
                pl.BlockSpec((BT, H_C, P_C, N_C), lambda i: (i, 0, 0, 0)),
                pl.BlockSpec((CONV_W - 1, BT, XBC_C), lambda i: (0, i, 0))]
    in_specs += [_const_spec(x) for x in consts]
    res = pl.pallas_call(
        _ssd_sample_kernel,
        grid=(nbs // BT,), in_specs=in_specs,
        out_specs=[pl.BlockSpec((BT, 512), lambda i: (i, 0)),
                   pl.BlockSpec((BT, H_C, P_C, N_C), lambda i: (i, 0, 0, 0)),
                   pl.BlockSpec((CONV_W - 1, BT, XBC_C), lambda i: (0, i, 0))],
        out_shape=[jax.ShapeDtypeStruct((nbs, 512), BF16), jax.ShapeDtypeStruct(h0.shape, F32),
                   jax.ShapeDtypeStruct((CONV_W - 1, nbs, XBC_C), F32)],
        scratch_shapes=[pltpu.VMEM((BT, W_C), F32), pltpu.VMEM((BT, N_C), F32), pltpu.VMEM((BT, N_C), F32),
                        pltpu.VMEM((BT, H_C * N_C), F32), pltpu.VMEM((BT, W_C), F32)],
        compiler_params=_cparams(1), name="ssd_sample",
    )(proj, proj, proj, h0, jnp.swapaxes(conv0, 0, 1), *consts)
    return res[0], res[1], jnp.swapaxes(res[2], 0, 1)


def _gla_sample_kernel(q_ref, k_ref, v_ref, gd_ref, sm_ref, s_ref, wlr_ref, blr_ref, seg_ref, ones_ref, g_ref,
                       out_ref, s_out, qe_scr, eg_scr, qs_scr):
    lg = _log_sigmoid(_doth(sm_ref[...], wlr_ref[...]) + blr_ref[...]) / GLA_TAU
    eg = jnp.exp(lg)
    qs = q_ref[...] * (DK_D ** -0.5)
    k = k_ref[...]
    v = v_ref[...]
    qe_scr[...] = qs * eg
    eg_scr[...] = eg
    ones8 = jnp.ones((8, DV_D), F32)

    def body(b, carry):
        for h in range(H_D):
            ks = slice(h * DK_D, (h + 1) * DK_D)
            vs = slice(h * DV_D, (h + 1) * DV_D)
            s_old = s_ref[b, h]
            qs_scr[pl.ds(b, 1), vs] = _doth(_row8(qe_scr, b, ks), s_old)[0:1, :]
            e_col = _doth(_first_row(_row8(eg_scr, b, ks)), ones8, TN)
            upd = _doth(_first_row(_row8(k_ref, b, ks)), _row8(v_ref, b, vs), TN)
            s_out[b, h] = e_col * s_old + upd
        return carry

    for b in range(BT):
        body(b, 0)
    o = _doth(qs * k, seg_ref[...]) * v + qs_scr[...]
    ms = _doth(o * o, ones_ref[...]) * (1.0 / DV_D)
    out_ref[...] = (o * lax.rsqrt(ms + EPS) * g_ref[...] * _silu(gd_ref[...])).astype(BF16)


def _gla_sample(proj, row0, nbs, s0, lw):
    wlr = jnp.zeros((128, H_D * DK_D), F32).at[16:32, :].set(lw['w_gla_lr'])
    consts = [wlr, lw['b_gla_lr'].reshape(1, -1), _block_ones(H_D, DK_D, DV_D), _block_ones(H_D, DV_D, DV_D),
              lw['norm_d_g'].reshape(1, -1)]
    in_specs = [_sample_spec(row0, 256, COL_QD), _sample_spec(row0, 256, COL_KD),
                _sample_spec(row0, 512, COL_VD), _sample_spec(row0, 512, COL_GD),
                _sample_spec(row0, 128, COL_SMALL),
                pl.BlockSpec((BT, H_D, DK_D, DV_D), lambda i: (i, 0, 0, 0))]
    in_specs += [_const_spec(x) for x in consts]
    res = pl.pallas_call(
        _gla_sample_kernel,
        grid=(nbs // BT,), in_specs=in_specs,
        out_specs=[pl.BlockSpec((BT, 512), lambda i: (i, 0)),
                   pl.BlockSpec((BT, H_D, DK_D, DV_D), lambda i: (i, 0, 0, 0))],
        out_shape=[jax.ShapeDtypeStruct((nbs, 512), BF16), jax.ShapeDtypeStruct(s0.shape, F32)],
        scratch_shapes=[pltpu.VMEM((BT, H_D * DK_D), F32), pltpu.VMEM((BT, H_D * DK_D), F32),
                        pltpu.VMEM((BT, H_D * DV_D), F32)],
        compiler_params=_cparams(1), name="gla_sample",
    )(proj, proj, proj, proj, proj, s0, *consts)
    return res[0], res[1]


def _rglru_sample_kernel(xb_ref, yb_ref, h_ref, conv_ref, cw_ref, cb_ref, wa_ref, ba_ref, wx_ref, bx_ref,
                         lam_ref, out_ref, h_out, conv_out):
    xc = _conv_step(conv_ref, xb_ref[...], cw_ref[...], cb_ref[...], conv_out)
    a, u = _rglru_gates(xc, wa_ref[...], ba_ref[...], wx_ref[...], bx_ref[...], _softplus(-lam_ref[...]))
    h = a * h_ref[...] + u
    h_out[...] = h
    out_ref[...] = (h * _gelu_tanh(yb_ref[...])).astype(BF16)


def _rglru_sample(proj, row0, nbs, h0, conv0, lw):
    consts = _rglru_consts(lw)
    in_specs = [_sample_spec(row0, 512, COL_XB), _sample_spec(row0, 512, COL_YB),
                pl.BlockSpec((BT, W_B), lambda i: (i, 0)),
                pl.BlockSpec((CONV_W - 1, BT, W_B), lambda i: (0, i, 0))]
    in_specs += [_const_spec(x) for x in consts]
    res = pl.pallas_call(
        _rglru_sample_kernel,
        grid=(nbs // BT,), in_specs=in_specs,
        out_specs=[pl.BlockSpec((BT, 512), lambda i: (i, 0)), pl.BlockSpec((BT, W_B), lambda i: (i, 0)),
                   pl.BlockSpec((CONV_W - 1, BT, W_B), lambda i: (0, i, 0))],
        out_shape=[jax.ShapeDtypeStruct((nbs, 512), BF16), jax.ShapeDtypeStruct((nbs, W_B), F32),
                   jax.ShapeDtypeStruct((CONV_W - 1, nbs, W_B), F32)],
        compiler_params=_cparams(1), name="rglru_sample",
    )(proj, proj, h0, jnp.swapaxes(conv0, 0, 1), *consts)
    return res[0], res[1], jnp.swapaxes(res[2], 0, 1)


def _pack_w_in(w):
    d = w.shape[0]
    qa_oa = w[:, 0:2048]
    iga_fga = w[:, 2048:2056]
    xb_yb_zc = w[:, 2056:3592]
    xbc = w[:, 3592:4360]
    dtc = w[:, 4360:4368]
    qd_kd = w[:, 4368:4880]
    vd_gd = w[:, 4880:5904]
    lrd = w[:, 5904:5920]
    pad = jnp.zeros((d, N_PROJ - COL_SMALL - 32), w.dtype)
    return jnp.concatenate([qa_oa, xb_yb_zc, vd_gd, xbc, qd_kd, iga_fga, dtc, lrd, pad], axis=1).astype(BF16)


MLSTM_L = 128
SSD_L = 128
GLA_L = 64
RGLRU_L = 256


def _layer(i, h, hn, p, sample_state, lw, ffn_w, w_router, g_next, next_dtype, s_len, nbs):
    t_p = NB * s_len
    proj = _in_proj(hn, lw['w_in_p'])
    small_t = proj[:t_p, COL_SMALL:COL_SMALL + 32].T
    mc0, mn0, mm0, rh0, rconv0, sh0, sconv0, gs0 = sample_state
    pa, p_mc, p_mn, p_mm = _mlstm_prompt(proj, small_t, s_len, lw['b_ig'], lw['b_fg'], lw['norm_a_g'], MLSTM_L)
    pb, p_rh, p_rconv = _rglru_prompt(proj, s_len, lw, RGLRU_L)
    pc, p_sh, p_sconv = _ssd_prompt(proj, small_t, s_len, lw, SSD_L)
    pd, p_gs = _gla_prompt(proj, s_len, lw, GLA_L)
    sa, s_mc, s_mn, s_mm = _mlstm_sample(proj, t_p, nbs, mc0, mn0, mm0, lw['b_ig'], lw['b_fg'], lw['norm_a_g'])
    sb, s_rh, s_rconv = _rglru_sample(proj, t_p, nbs, rh0, rconv0, lw)
    sc, s_sh, s_sconv = _ssd_sample(proj, t_p, nbs, sh0, sconv0, lw)
    sd, s_gs = _gla_sample(proj, t_p, nbs, gs0, lw)
    mixes = [jnp.concatenate(list(pp) + [ss], axis=0) for pp, ss in ((pa, sa), (pb, sb), (pc, sc), (pd, sd))]
    res = _out_proj(mixes, lw['w_out'], h, lw['norm_ffn_g'], w_router)
    if w_router is not None:
        h, hn2, comb = res
    else:
        (h, hn2), comb = res, None
    h, hn3 = _ffn(hn2, ffn_w[0], ffn_w[1], ffn_w[2], h, lw['norm_ple_g'], comb)
    h, nxt = _ple(hn3, p, h, lw['w_ple_gate'], lw['w_ple_up'], g_next, next_dtype)
    p_state = (p_mc, p_mn, p_mm, p_rh, p_rconv, p_sh, p_sconv, p_gs)
    s_state = (s_mc, s_mn, s_mm, s_rh, s_rconv, s_sh, s_sconv, s_gs)
    return h, nxt, p_state, s_state


def kernel(x_prompt, x_sample, state_mlstm_c, state_mlstm_n, state_mlstm_m, state_rglru_h, state_rglru_conv,
           state_ssd_h, state_ssd_conv, state_gla_s, p_prompt, p_sample, norm_mix_g, w_in, b_ig, b_fg,
           norm_a_g, conv_b_w, conv_b_b, w_rg_a, b_rg_a, w_rg_x, b_rg_x, lru_lambda, conv_c_w, conv_c_b,
           dt_bias, a_log, d_skip, norm_c_g, w_gla_lr, b_gla_lr, norm_d_g, w_out, norm_ffn_g, w_gate_dense,
           w_up_dense, w_down_dense, w_router, w_gate_moe, w_up_moe, w_down_moe, norm_ple_g, w_ple_gate,
           w_ple_up, final_norm_g):
    depth = w_in.shape[0]
    nb, s_len, d = x_prompt.shape
    assert nb == NB
    nbs = x_sample.shape[0]
    t_p = nb * s_len
    stacked = dict(b_ig=b_ig, b_fg=b_fg, norm_a_g=norm_a_g, conv_b_w=conv_b_w, conv_b_b=conv_b_b,
                   w_rg_a=w_rg_a, b_rg_a=b_rg_a, w_rg_x=w_rg_x, b_rg_x=b_rg_x, lru_lambda=lru_lambda,
                   conv_c_w=conv_c_w, conv_c_b=conv_c_b, dt_bias=dt_bias, a_log=a_log, d_skip=d_skip,
                   norm_c_g=norm_c_g, w_gla_lr=w_gla_lr, b_gla_lr=b_gla_lr, norm_d_g=norm_d_g,
                   norm_ffn_g=norm_ffn_g, norm_ple_g=norm_ple_g)
    states = (state_mlstm_c, state_mlstm_n, state_mlstm_m, state_rglru_h, state_rglru_conv, state_ssd_h,
              state_ssd_conv, state_gla_s)
    h = jnp.concatenate([x_prompt.reshape(t_p, d), x_sample.reshape(nbs, d)], axis=0)
    nxt = _rmsnorm(h, norm_mix_g[0], BF16)
    p_new = [[] for _ in range(8)]
    s_new = [[] for _ in range(8)]
    for i in range(depth):
        lw = {name: arr[i] for name, arr in stacked.items()}
        lw['w_in_p'] = _pack_w_in(w_in[i])
        lw['w_out'] = w_out[i].astype(BF16)
        lw['w_ple_gate'] = w_ple_gate[i].astype(BF16)
        lw['w_ple_up'] = w_ple_up[i].astype(BF16)
        if i % 2 == 1:
            j = i // 2
            ffn_w = (w_gate_moe[j].astype(BF16), w_up_moe[j].astype(BF16), w_down_moe[j].astype(BF16))
            wr = jnp.pad(w_router[j], ((0, 0), (0, 128 - N_EXP)))
        else:
            j = i // 2
            ffn_w = (w_gate_dense[j:j + 1].astype(BF16), w_up_dense[j:j + 1].astype(BF16),
                     w_down_dense[j:j + 1].astype(BF16))
            wr = None
        p = jnp.concatenate([p_prompt[i].reshape(t_p, -1), p_sample[i].reshape(nbs, -1)], axis=0).astype(BF16)
        last = i == depth - 1
        g_next = final_norm_g if last else norm_mix_g[i + 1]
        h, nxt, p_st, s_st = _layer(i, h, nxt, p, tuple(s[i] for s in states), lw, ffn_w, wr, g_next,
                                    F32 if last else BF16, s_len, nbs)
        for j in range(8):
            p_new[j].append(p_st[j])
            s_new[j].append(s_st[j])
    y_prompt = nxt[:t_p].reshape(nb, s_len, d)
    y_sample = nxt[t_p:].reshape(nbs, 1, d)
    return (y_prompt, y_sample) + tuple(jnp.stack(x) for x in p_new) + tuple(jnp.stack(x) for x in s_new)
```
